```python
import jax, jax.numpy as jnp
from jax import lax
import numpy as np

D_MODEL = 4096
BATCH = 1
SEQ = 16384
DEPTH = 1

HG_WIDTH = D_MODEL // 2
CONV_WIDTH = D_MODEL - HG_WIDTH
HG_EXPAND = 128
HG_HEADS = HG_WIDTH // HG_EXPAND
HG_VDIM = HG_WIDTH // HG_HEADS
HG_CHUNK = 64
CONV_KERNEL = 31
IN_COLS = 4 * HG_WIDTH + 2 * CONV_WIDTH
PEER_HEADS = 8
PEER_QDIM = 256
PEER_HALF = PEER_QDIM // 2
N_KEYS = 128
N_EXPERTS = N_KEYS * N_KEYS
PEER_TOPK = 16
PEER_BLOCK = 128
ADA_SLOTS = 6
NORM_EPS = 1e-6
LN_EPS = 1e-5

kernel_name = 'hymba_style_hgrn2_conformer_peer_adaln'


def rms_norm(x, gain):
    xf = x.astype(jnp.float32)
    y = xf * lax.rsqrt(jnp.mean(xf * xf, axis=-1, keepdims=True) + NORM_EPS)
    return (y * gain.astype(jnp.float32)).astype(x.dtype)


def modulate(h, shift, scale):
    return h * (1 + scale[:, None, :]) + shift[:, None, :]


def hgrn2_group(q, fz, i, g, lb, norm_g):
    B, S, _ = q.shape
    f32 = jnp.float32
    qf = jax.nn.silu(q.astype(f32))
    lbf = lb.astype(f32)
    f = lbf + (1 - lbf) * jax.nn.sigmoid(fz.astype(f32))
    k = 1 - f
    logf = jnp.log(f)
    v = i.astype(f32)
    nc = S // HG_CHUNK

    def chunks(t, d):
        return t.reshape(B, nc, HG_CHUNK, HG_HEADS, d).transpose(1, 0, 3, 2, 4)

    qc = chunks(qf, HG_EXPAND)
    kc = chunks(k, HG_EXPAND)
    vc = chunks(v, HG_VDIM)
    bc = jnp.cumsum(chunks(logf, HG_EXPAND), axis=3)
    causal = jnp.tril(jnp.ones((HG_CHUNK, HG_CHUNK), dtype=bool))

    def step(state, inp):
        qt, kt, vt, bt = inp
        diff = bt[:, :, :, None, :] - bt[:, :, None, :, :]
        decay = jnp.exp(jnp.where(causal[:, :, None], diff, -jnp.inf))
        scores = jnp.einsum('bhtk,bhsk,bhtsk->bhts', qt, kt, decay)
        o = jnp.einsum('bhts,bhsv->bhtv', scores, vt) + jnp.einsum('bhtk,bhkv->bhtv', qt * jnp.exp(bt), state)
        blast = bt[:, :, -1:, :]
        state = jnp.exp(blast[:, :, 0, :])[..., None] * state + jnp.einsum('bhsk,bhsv->bhkv', kt * jnp.exp(blast - bt), vt)
        return state, o

    state0 = jnp.zeros((B, HG_HEADS, HG_EXPAND, HG_VDIM), f32)
    _, o = lax.scan(step, state0, (qc, kc, vc, bc))
    o = o.transpose(1, 0, 3, 2, 4).reshape(B, S, HG_HEADS, HG_VDIM)
    o = o * lax.rsqrt(jnp.mean(o * o, axis=-1, keepdims=True) + NORM_EPS)
    o = o * norm_g.astype(f32).reshape(HG_HEADS, HG_VDIM)
    o = o.reshape(B, S, HG_WIDTH) * jax.nn.silu(g.astype(f32))
    return o.astype(q.dtype)


def conformer_conv_group(a, b, w_dw, b_dw, ln_g, ln_b):
    u = a * jax.nn.sigmoid(b)
    y = lax.conv_general_dilated(
        u, w_dw[:, None, :].astype(u.dtype), window_strides=(1,),
        padding=[(CONV_KERNEL - 1, 0)], dimension_numbers=('NWC', 'WIO', 'NWC'),
        feature_group_count=CONV_WIDTH) + b_dw
    yf = y.astype(jnp.float32)
    mu = jnp.mean(yf, axis=-1, keepdims=True)
    var = jnp.mean(jnp.square(yf - mu), axis=-1, keepdims=True)
    yn = (yf - mu) * lax.rsqrt(var + LN_EPS) * ln_g.astype(jnp.float32) + ln_b.astype(jnp.float32)
    return jax.nn.silu(yn).astype(a.dtype)


def peer_layer(h, w_pq, keys1, keys2, u, v):
    B, S, D = h.shape
    T = B * S
    hf = h.reshape(T, D)
    q = (hf @ w_pq).reshape(T, PEER_HEADS, 2, PEER_HALF).astype(jnp.float32)
    s1 = jnp.einsum('thd,hnd->thn', q[:, :, 0], keys1.astype(jnp.float32))
    s2 = jnp.einsum('thd,hnd->thn', q[:, :, 1], keys2.astype(jnp.float32))
    v1, i1 = lax.top_k(s1, PEER_TOPK)
    v2, i2 = lax.top_k(s2, PEER_TOPK)
    cand_s = (v1[..., :, None] + v2[..., None, :]).reshape(T, PEER_HEADS, PEER_TOPK * PEER_TOPK)
    cand_i = (i1[..., :, None] * N_KEYS + i2[..., None, :]).reshape(T, PEER_HEADS, PEER_TOPK * PEER_TOPK)
    top_s, pos = lax.top_k(cand_s, PEER_TOPK)
    idx = jnp.take_along_axis(cand_i, pos, axis=-1)
    gate = jax.nn.softmax(top_s, axis=-1).astype(h.dtype)
    n_sel = PEER_HEADS * PEER_TOPK
    nb = T // PEER_BLOCK

    def block(args):
        xb, ib, gb = args
        a = jnp.einsum('td,tkd->tk', xb, u[ib])
        return jnp.einsum('tk,tkd->td', jax.nn.gelu(a, approximate=False) * gb, v[ib])

    y = lax.map(block, (hf.reshape(nb, PEER_BLOCK, D), idx.reshape(nb, PEER_BLOCK, n_sel),
                        gate.reshape(nb, PEER_BLOCK, n_sel)))
    return y.reshape(B, S, D)


def setup_inputs(seed: int = 0) -> dict:
    key = jax.random.key(seed)
    ks = jax.random.split(key, 24)
    f32 = jnp.float32

    def nrm(k, shape, s):
        return jax.random.normal(k, shape, f32) * s

    D = D_MODEL
    return {
        'x': nrm(ks[0], (BATCH, SEQ, D), 1.0),
        'c': nrm(ks[1], (BATCH, D), 1.0),
        'w_ada': nrm(ks[2], (DEPTH, D, ADA_SLOTS * D), 0.5 * D ** -0.5),
        'b_ada': nrm(ks[3], (DEPTH, ADA_SLOTS * D), 0.01),
        'w_ada_out': nrm(ks[4], (D, 2 * D), 0.5 * D ** -0.5),
        'b_ada_out': nrm(ks[5], (2 * D,), 0.01),
        'g_mix': 1.0 + nrm(ks[6], (DEPTH, D), 0.01),
        'g_ffn': 1.0 + nrm(ks[7], (DEPTH, D), 0.01),
        'g_out': 1.0 + nrm(ks[8], (D,), 0.01),
        'w_in': nrm(ks[9], (DEPTH, D, IN_COLS), D ** -0.5),
        'b_glu': nrm(ks[10], (DEPTH, 2 * CONV_WIDTH), 0.01),
        'lb_logits': nrm(ks[11], (DEPTH + 1, HG_WIDTH), 0.1),
        'w_dw': nrm(ks[12], (DEPTH, CONV_KERNEL, CONV_WIDTH), CONV_KERNEL ** -0.5),
        'b_dw': nrm(ks[13], (DEPTH, CONV_WIDTH), 0.01),
        'ln_g': 1.0 + nrm(ks[14], (DEPTH, CONV_WIDTH), 0.01),
        'ln_b': nrm(ks[15], (DEPTH, CONV_WIDTH), 0.01),
        'hg_norm_g': 1.0 + nrm(ks[16], (DEPTH, HG_WIDTH), 0.01),
        'w_out': nrm(ks[17], (DEPTH, HG_WIDTH + CONV_WIDTH, D), (HG_WIDTH + CONV_WIDTH) ** -0.5),
        'w_pq': nrm(ks[18], (DEPTH, D, PEER_HEADS * PEER_QDIM), D ** -0.5),
        'sub_keys1': nrm(ks[19], (DEPTH, PEER_HEADS, N_KEYS, PEER_HALF), PEER_HALF ** -0.5),
        'sub_keys2': nrm(ks[20], (DEPTH, PEER_HEADS, N_KEYS, PEER_HALF), PEER_HALF ** -0.5),
        'expert_u': nrm(ks[21], (DEPTH, N_EXPERTS, D), D ** -0.5),
        'expert_v': nrm(ks[22], (DEPTH, N_EXPERTS, D), 1.0),
    }


def reference(x, c, w_ada, b_ada, w_ada_out, b_ada_out, g_mix, g_ffn, g_out, w_in, b_glu,
              lb_logits, w_dw, b_dw, ln_g, ln_b, hg_norm_g, w_out, w_pq, sub_keys1, sub_keys2,
              expert_u, expert_v):
    c_act = jax.nn.silu(c)
    lb_all = jnp.cumsum(jax.nn.softmax(lb_logits.astype(jnp.float32), axis=0), axis=0)
    split_pts = [HG_WIDTH, 2 * HG_WIDTH, 3 * HG_WIDTH, 4 * HG_WIDTH]
    for l in range(DEPTH):
        mod = c_act @ w_ada[l] + b_ada[l]
        sh_m, sc_m, gt_m, sh_f, sc_f, gt_f = jnp.split(mod, ADA_SLOTS, axis=-1)
        h = modulate(rms_norm(x, g_mix[l]), sh_m, sc_m)
        proj = jnp.einsum('bsd,de->bse', h, w_in[l])
        qh, fh, ih, gh, conv_in = jnp.split(proj, split_pts, axis=-1)
        conv_a, conv_b = jnp.split(conv_in + b_glu[l], 2, axis=-1)
        y_hg = hgrn2_group(qh, fh, ih, gh, lb_all[l], hg_norm_g[l])
        y_cv = conformer_conv_group(conv_a, conv_b, w_dw[l], b_dw[l], ln_g[l], ln_b[l])
        y_mix = jnp.einsum('bse,ed->bsd', jnp.concatenate([y_hg, y_cv], axis=-1), w_out[l])
        x = x + gt_m[:, None, :] * y_mix
        h = modulate(rms_norm(x, g_ffn[l]), sh_f, sc_f)
        y_ffn = peer_layer(h, w_pq[l], sub_keys1[l], sub_keys2[l], expert_u[l], expert_v[l])
        x = x + gt_f[:, None, :] * y_ffn
    mod_o = c_act @ w_ada_out + b_ada_out
    sh_o, sc_o = jnp.split(mod_o, 2, axis=-1)
    return modulate(rms_norm(x, g_out), sh_o, sc_o)
```

```python
import functools

import jax
import jax.numpy as jnp
from jax import lax
from jax.experimental import pallas as pl
from jax.experimental.pallas import tpu as pltpu

F32 = jnp.float32
BF16 = jnp.bfloat16

NORM_EPS = 1e-6
LN_EPS = 1e-5
HEAD_DIM = 128
CONV_KERNEL = 31
CONV_HALO = 32
PEER_HEADS = 8
PEER_HALF = 128
N_KEYS = 128
PEER_TOPK = 16
LANES = 128
SUBLANES = 8
VMEM_LIMIT = 56 * 1024 * 1024

NT_DIMS = (((1,), (1,)), ((), ()))
TN_DIMS = (((0,), (0,)), ((), ()))


def _sigmoid(x):
    return 1.0 / (1.0 + jnp.exp(-x))


def _silu(x):
    return x * _sigmoid(x)


def _params(sem):
    return pltpu.CompilerParams(dimension_semantics=sem, vmem_limit_bytes=VMEM_LIMIT)


def _tile(n, pref, unit=LANES):
    if n <= pref:
        return n
    best = unit
    for cand in range(unit, pref + 1, unit):
        if n % cand == 0:
            best = cand
    assert n % best == 0, (n, pref, unit)
    return best


def _ada_kernel(c_ref, w_ref, b_ref, o_ref, *, rows):
    d, tn = w_ref.shape

    def body(i, acc):
        r = pl.multiple_of(i * rows, rows)
        cc = _silu(c_ref[pl.ds(r, rows), :])
        blk = w_ref[pl.ds(r, rows), :] * cc
        return acc + jnp.sum(blk.reshape(rows // SUBLANES, SUBLANES, tn), axis=0)

    acc = lax.fori_loop(0, d // rows, body, jnp.zeros((SUBLANES, tn), F32))
    o_ref[...] = jnp.sum(acc, axis=0, keepdims=True) + b_ref[...]


def _ada(c_col, w, b, *, tn=512, rows=128):
    d, n = w.shape
    return pl.pallas_call(
        functools.partial(_ada_kernel, rows=rows),
        out_shape=jax.ShapeDtypeStruct((1, n), F32),
        grid=(n // tn,),
        in_specs=[
            pl.BlockSpec((d, 1), lambda j: (0, 0)),
            pl.BlockSpec((d, tn), lambda j: (0, j)),
            pl.BlockSpec((1, tn), lambda j: (0, j)),
        ],
        out_specs=pl.BlockSpec((1, tn), lambda j: (0, j)),
        compiler_params=_params(("parallel",)),
        name="ada",
    )(c_col, w, b)


def _norm_mod_rows(x_ref, g_ref, sh_ref, sc_ref, dst_ref, *, rows, res_ref=None, gate_ref=None):
    tm = x_ref.shape[0]
    g = g_ref[...]
    scale1 = 1.0 + sc_ref[...]
    sh = sh_ref[...]

    def body(i, carry):
        r = pl.multiple_of(i * rows, rows)
        x = x_ref[pl.ds(r, rows), :]
        if res_ref is not None:
            x = x + gate_ref[...] * res_ref[pl.ds(r, rows), :]
        ms = jnp.mean(x * x, axis=-1, keepdims=True)
        y = x * lax.rsqrt(ms + NORM_EPS) * g
        dst_ref[pl.ds(r, rows), :] = (y * scale1 + sh).astype(dst_ref.dtype)
        return carry

    lax.fori_loop(0, tm // rows, body, 0)


def _nmm_kernel(x_ref, g_ref, sh_ref, sc_ref, w_ref, b_ref, o_ref, h_ref, *, rows):
    @pl.when(pl.program_id(1) == 0)
    def _():
        _norm_mod_rows(x_ref, g_ref, sh_ref, sc_ref, h_ref, rows=rows)

    o_ref[...] = jnp.dot(h_ref[...], w_ref[...], preferred_element_type=F32) + b_ref[...]


def _norm_mod_matmul(x, g, sh, sc, w, b, *, emit_h, tm=512, tn=1024, rows=32):
    t, d = x.shape
    n = w.shape[1]
    tm = _tile(t, tm, SUBLANES)
    tn = _tile(n, tn)
    row_spec = pl.BlockSpec((1, d), lambda i, j: (0, 0))
    out_shape = [jax.ShapeDtypeStruct((t, n), F32)]
    out_specs = [pl.BlockSpec((tm, tn), lambda i, j: (i, j))]
    scratch = []
    if emit_h:
        out_shape.append(jax.ShapeDtypeStruct((t, d), BF16))
        out_specs.append(pl.BlockSpec((tm, d), lambda i, j: (i, 0)))
    else:
        scratch.append(pltpu.VMEM((tm, d), BF16))
    return pl.pallas_call(
        functools.partial(_nmm_kernel, rows=rows),
        out_shape=out_shape,
        grid=(t // tm, n // tn),
        in_specs=[
            pl.BlockSpec((tm, d), lambda i, j: (i, 0)),
            row_spec, row_spec, row_spec,
            pl.BlockSpec((d, tn), lambda i, j: (0, j)),
            pl.BlockSpec((1, tn), lambda i, j: (0, j)),
        ],
        out_specs=out_specs,
        scratch_shapes=scratch,
        compiler_params=_params(("parallel", "arbitrary")),
        name="nmm",
    )(x, g, sh, sc, w, b)


def _anchor_rows(b, m):
    c = b.shape[0]
    blk = 2 * m
    if blk >= SUBLANES:
        a = b.reshape(c // blk, blk, HEAD_DIM)[:, m - 1:m, :]
        return jnp.broadcast_to(a, (c // blk, blk, HEAD_DIM)).reshape(c, HEAD_DIM)
    pos = lax.broadcasted_iota(jnp.int32, b.shape, 0) & (blk - 1)
    out = b
    for off in range(-(m - 1), m + 1):
        if off == 0:
            continue
        shifted = pltpu.roll(b, off % c, 0)
        out = jnp.where(pos == (m - 1 + off), shifted, out)
    return out


def _hgrn_chunk(q_raw, fz, v, g_raw, lb, ng, st):
    c = q_raw.shape[0]
    q = _silu(q_raw)
    f = lb + (1.0 - lb) * _sigmoid(fz)
    k = 1.0 - f
    logf = jnp.log(f)
    row = lax.broadcasted_iota(jnp.int32, (c, HEAD_DIM), 0)

    b = logf
    s = 1
    while s < c:
        b = b + jnp.where(row >= s, pltpu.roll(b, s, 0), 0.0)
        s *= 2

    rr = lax.broadcasted_iota(jnp.int32, (c, c), 0)
    cc = lax.broadcasted_iota(jnp.int32, (c, c), 1)
    rc_xor = rr ^ cc
    scores = jnp.zeros((c, c), F32)
    m = 1
    while m < c:
        blk = 2 * m
        lower = (row & (blk - 1)) >= m
        bp = _anchor_rows(b, m)
        e = jnp.where(lower, b - bp, bp - b)
        dec = jnp.exp(e)
        xq = jnp.where(lower, q * dec, 0.0).astype(BF16)
        xk = jnp.where(lower, 0.0, k * dec).astype(BF16)
        s_l = lax.dot_general(xq, xk, NT_DIMS, preferred_element_type=F32)
        scores = scores + jnp.where(rc_xor < blk, s_l, 0.0)
        m *= 2

    vb = v.astype(BF16)
    o = jnp.sum(q * k, axis=-1, keepdims=True) * v
    o = o + jnp.dot(scores.astype(BF16), vb, preferred_element_type=F32)
    qd = (q * jnp.exp(b)).astype(BF16)
    o = o + lax.dot_general(qd, st.astype(BF16), NT_DIMS, preferred_element_type=F32)

    blast = b[c - 1:c, :]
    kd = (k * jnp.exp(blast - b)).astype(BF16)
    st_new = st * jnp.exp(blast) + lax.dot_general(vb, kd, TN_DIMS, preferred_element_type=F32)

    o = o * lax.rsqrt(jnp.mean(o * o, axis=-1, keepdims=True) + NORM_EPS)
    y = o * ng * _silu(g_raw)
    return y, st_new


def _hgrn_kernel(q_ref, f_ref, i_ref, g_ref, lbl_ref, ng_ref, o_ref, st_ref, *, chunk):
    @pl.when(pl.program_id(1) == 0)
    def _():
        st_ref[...] = jnp.zeros_like(st_ref)

    lbl = lbl_ref[...]
    ex = jnp.exp(lbl - jnp.max(lbl, axis=0, keepdims=True))
    lb = ex[0:1, :] / jnp.sum(ex, axis=0, keepdims=True)
    ng = ng_ref[...]
    tb = q_ref.shape[0]

    def body(ci, carry):
        r = pl.multiple_of(ci * chunk, chunk)
        sl = pl.ds(r, chunk)
        y, st_new = _hgrn_chunk(q_ref[sl, :], f_ref[sl, :], i_ref[sl, :], g_ref[sl, :], lb, ng, st_ref[...])
        st_ref[...] = st_new
        o_ref[sl, :] = y.astype(o_ref.dtype)
        return carry

    lax.fori_loop(0, tb // chunk, body, 0)


def _hgrn2(proj, lb_logits, norm_g, *, d_model, tb=1024, chunk=256):
    t = proj.shape[0]
    heads = (d_model // 2) // HEAD_DIM
    tb = min(tb, t)
    chunk = min(chunk, tb)
    nl = lb_logits.shape[0]

    def col(off):
        return pl.BlockSpec((tb, HEAD_DIM), lambda h, i: (i, off * heads + h))

    return pl.pallas_call(
        functools.partial(_hgrn_kernel, chunk=chunk),
        out_shape=jax.ShapeDtypeStruct((t, d_model), BF16),
        grid=(heads, t // tb),
        in_specs=[
            col(0), col(1), col(2), col(3),
            pl.BlockSpec((nl, HEAD_DIM), lambda h, i: (0, h)),
            pl.BlockSpec((1, HEAD_DIM), lambda h, i: (0, h)),
        ],
        out_specs=pl.BlockSpec((tb, HEAD_DIM), lambda h, i: (i, h)),
        scratch_shapes=[pltpu.VMEM((HEAD_DIM, HEAD_DIM), F32)],
        compiler_params=_params(("parallel", "arbitrary")),
        name="hgrn2",
    )(proj, proj, proj, proj, lb_logits, norm_g)


def _conv_kernel(a_ref, b_ref, w_ref, bdw_ref, lng_ref, lnb_ref, ycat_ref, o_ref, ubuf, ybuf, *, rt):
    del ycat_ref
    tb, cw = a_ref.shape
    i = pl.program_id(0)

    @pl.when(i == 0)
    def _():
        ubuf[0:CONV_HALO, :] = jnp.zeros((CONV_HALO, cw), F32)

    @pl.when(i > 0)
    def _():
        ubuf[0:CONV_HALO, :] = ubuf[tb:tb + CONV_HALO, :]

    def glu(ri, carry):
        r = pl.multiple_of(ri * rt, rt)
        ubuf[pl.ds(CONV_HALO + r, rt), :] = a_ref[pl.ds(r, rt), :] * _sigmoid(b_ref[pl.ds(r, rt), :])
        return carry

    lax.fori_loop(0, tb // rt, glu, 0)

    lead = CONV_HALO - (CONV_KERNEL - 1)

    def col_tile(ct, carry):
        c0 = pl.multiple_of(ct * LANES, LANES)
        cs = pl.ds(c0, LANES)
        w = w_ref[:, cs]
        bias = bdw_ref[:, cs]

        def row_tile(ri, carry2):
            r = pl.multiple_of(ri * rt, rt)
            win = ubuf[pl.ds(r, rt + CONV_HALO), cs]
            acc = jnp.zeros((rt, LANES), F32) + bias
            for j in range(CONV_KERNEL):
                acc = acc + w[j:j + 1, :] * win[lead + j:lead + j + rt, :]
            ybuf[pl.ds(r, rt), cs] = acc
            return carry2

        lax.fori_loop(0, tb // rt, row_tile, 0)
        return carry

    lax.fori_loop(0, cw // LANES, col_tile, 0)

    lng = lng_ref[...]
    lnb = lnb_ref[...]

    def ln(ri, carry):
        r = pl.multiple_of(ri * rt, rt)
        y = ybuf[pl.ds(r, rt), :]
        mu = jnp.mean(y, axis=-1, keepdims=True)
        yc = y - mu
        var = jnp.mean(yc * yc, axis=-1, keepdims=True)
        yn = yc * lax.rsqrt(var + LN_EPS) * lng + lnb
        o_ref[pl.ds(r, rt), :] = _silu(yn).astype(o_ref.dtype)
        return carry

    lax.fori_loop(0, tb // rt, ln, 0)


def _conv_group(proj, w_dw, b_dw, ln_g, ln_b, ycat, *, d_model, tb=256, rt=32):
    t = proj.shape[0]
    cw = d_model // 2
    tb = min(tb, t)
    row = pl.BlockSpec((1, cw), lambda i: (0, 0))
    return pl.pallas_call(
        functools.partial(_conv_kernel, rt=rt),
        out_shape=jax.ShapeDtypeStruct(ycat.shape, ycat.dtype),
        grid=(t // tb,),
        in_specs=[
            pl.BlockSpec((tb, cw), lambda i: (i, 4)),
            pl.BlockSpec((tb, cw), lambda i: (i, 5)),
            pl.BlockSpec((CONV_KERNEL, cw), lambda i: (0, 0)),
            row, row, row,
            pl.BlockSpec(memory_space=pl.ANY),
        ],
        out_specs=pl.BlockSpec((tb, cw), lambda i: (i, 1)),
        scratch_shapes=[pltpu.VMEM((tb + CONV_HALO, cw), F32), pltpu.VMEM((tb, cw), F32)],
        input_output_aliases={6: 0},
        compiler_params=_params(("arbitrary",)),
        name="conv",
    )(proj, proj, w_dw, b_dw, ln_g, ln_b, ycat)


def _oproj_kernel(y_ref, w_ref, x_ref, gate_ref, o_ref):
    acc = jnp.dot(y_ref[...], w_ref[...], preferred_element_type=F32)
    o_ref[...] = x_ref[...] + gate_ref[...] * acc


def _oproj(ycat, w, x, gate, *, tm=1024, tn=1024):
    t, kd = ycat.shape
    n = w.shape[1]
    tm = _tile(t, tm, SUBLANES)
    tn = _tile(n, tn)
    return pl.pallas_call(
        _oproj_kernel,
        out_shape=jax.ShapeDtypeStruct((t, n), F32),
        grid=(t // tm, n // tn),
        in_specs=[
            pl.BlockSpec((tm, kd), lambda i, j: (i, 0)),
            pl.BlockSpec((kd, tn), lambda i, j: (0, j)),
            pl.BlockSpec((tm, tn), lambda i, j: (i, j)),
            pl.BlockSpec((1, tn), lambda i, j: (0, j)),
        ],
        out_specs=pl.BlockSpec((tm, tn), lambda i, j: (i, j)),
        compiler_params=_params(("parallel", "arbitrary")),
        name="oproj",
    )(ycat, w, x, gate)


def _extract_desc(s, n):
    vals = []
    for _ in range(n):
        m = jnp.max(s, axis=0, keepdims=True)
        vals.append(m)
        s = jnp.where(s == m, -jnp.inf, s)
    return vals


def _router_kernel(q_ref, k1_ref, k2_ref, s1_ref, e1_ref, s2_ref, e2_ref, tau_ref):
    q = q_ref[...]
    s1 = lax.dot_general(k1_ref[0], q[:, :PEER_HALF], NT_DIMS, precision=lax.Precision.HIGHEST,
                         preferred_element_type=F32)
    s2 = lax.dot_general(k2_ref[0], q[:, PEER_HALF:], NT_DIMS, precision=lax.Precision.HIGHEST,
                         preferred_element_type=F32)
    v1 = _extract_desc(s1, PEER_TOPK)
    v2 = _extract_desc(s2, PEER_TOPK)
    cand = [v1[a] + v2[b] for a in range(PEER_TOPK) for b in range(PEER_TOPK)
            if (a + 1) * (b + 1) <= PEER_TOPK]
    pad = (-len(cand)) % SUBLANES
    cand += [jnp.full_like(cand[0], -jnp.inf)] * pad
    top = _extract_desc(jnp.concatenate(cand, axis=0), PEER_TOPK)
    z = jnp.zeros_like(top[0])
    for m in top:
        z = z + jnp.exp(m - top[0])
    s1_ref[0] = s1
    s2_ref[0] = s2
    e1_ref[0] = jnp.exp(s1 - v1[0]) / z
    e2_ref[0] = jnp.exp(s2 - v2[0])
    tau_ref[0] = top[PEER_TOPK - 1]


def _router(q, keys1, keys2, *, tt=256):
    t = q.shape[0]
    tt = min(tt, t)
    big = jax.ShapeDtypeStruct((PEER_HEADS, N_KEYS, t), F32)
    key_spec = pl.BlockSpec((1, N_KEYS, PEER_HALF), lambda i, h: (h, 0, 0))
    big_spec = pl.BlockSpec((1, N_KEYS, tt), lambda i, h: (h, 0, i))
    return pl.pallas_call(
        _router_kernel,
        out_shape=[big, big, big, big, jax.ShapeDtypeStruct((PEER_HEADS, 1, t), F32)],
        grid=(t // tt, PEER_HEADS),
        in_specs=[pl.BlockSpec((tt, 2 * PEER_HALF), lambda i, h: (i, h)), key_spec, key_spec],
        out_specs=[big_spec, big_spec, big_spec, big_spec, pl.BlockSpec((1, 1, tt), lambda i, h: (h, 0, i))],
        compiler_params=_params(("parallel", "parallel")),
        name="router",
    )(q, keys1, keys2)


def _peer_dense_kernel(h_ref, u_ref, v_ref, s1_ref, e1_ref, s2_ref, e2_ref, tau_ref, y_ref):
    eb = pl.program_id(1)
    n_i1 = u_ref.shape[0] // N_KEYS

    @pl.when(eb == 0)
    def _():
        y_ref[...] = jnp.zeros_like(y_ref)

    a_t = lax.dot_general(u_ref[...], h_ref[...], NT_DIMS, preferred_element_type=F32)
    act = 0.5 * a_t * (1.0 + lax.erf(a_t * 0.7071067811865476))

    parts = []
    for il in range(n_i1):
        w = jnp.zeros((N_KEYS, a_t.shape[1]), F32)
        for hh in range(PEER_HEADS):
            i1 = eb * n_i1 + il
            s1row = s1_ref[hh, pl.ds(i1, 1), :]
            e1row = e1_ref[hh, pl.ds(i1, 1), :]
            sel = (s1row + s2_ref[hh]) >= tau_ref[hh]
            w = w + jnp.where(sel, e1row * e2_ref[hh], 0.0)
        parts.append(w)
    wt = parts[0] if n_i1 == 1 else jnp.concatenate(parts, axis=0)
    g_t = (act * wt).astype(BF16)
    y_ref[...] += lax.dot_general(g_t, v_ref[...], TN_DIMS, preferred_element_type=F32)


def _peer_dense(h, u, v, s1, e1, s2, e2, tau, *, tb=512, eb=256):
    t, d = h.shape
    n_exp = u.shape[0]
    tb = min(tb, t)
    big_spec = pl.BlockSpec((PEER_HEADS, N_KEYS, tb), lambda i, e: (0, 0, i))
    return pl.pallas_call(
        _peer_dense_kernel,
        out_shape=jax.ShapeDtypeStruct((t, d), F32),
        grid=(t // tb, n_exp // eb),
        in_specs=[
            pl.BlockSpec((tb, d), lambda i, e: (i, 0)),
            pl.BlockSpec((eb, d), lambda i, e: (e, 0)),
            pl.BlockSpec((eb, d), lambda i, e: (e, 0)),
            big_spec, big_spec, big_spec, big_spec,
            pl.BlockSpec((PEER_HEADS, 1, tb), lambda i, e: (0, 0, i)),
        ],
        out_specs=pl.BlockSpec((tb, d), lambda i, e: (i, 0)),
        compiler_params=_params(("parallel", "arbitrary")),
        name="peer_dense",
    )(h, u, v, s1, e1, s2, e2, tau)


def _final_kernel(x_ref, y_ref, gate_ref, g_ref, sh_ref, sc_ref, o_ref, *, rows):
    _norm_mod_rows(x_ref, g_ref, sh_ref, sc_ref, o_ref, rows=rows, res_ref=y_ref, gate_ref=gate_ref)


def _final(x, y, gate, g, sh, sc, *, tm=256, rows=32):
    t, d = x.shape
    tm = min(tm, t)
    row = pl.BlockSpec((1, d), lambda i: (0, 0))
    blk = pl.BlockSpec((tm, d), lambda i: (i, 0))
    return pl.pallas_call(
        functools.partial(_final_kernel, rows=rows),
        out_shape=jax.ShapeDtypeStruct((t, d), F32),
        grid=(t // tm,),
        in_specs=[blk, blk, row, row, row, row],
        out_specs=blk,
        compiler_params=_params(("parallel",)),
        name="final",
    )(x, y, gate, g, sh, sc)


def kernel(x, c, w_ada, b_ada, w_ada_out, b_ada_out, g_mix, g_ffn, g_out, w_in, b_glu, lb_logits, w_dw, b_dw,
           ln_g, ln_b, hg_norm_g, w_out, w_pq, sub_keys1, sub_keys2, expert_u, expert_v):
    bsz, seq, d = x.shape
    assert bsz == 1 and w_ada.shape[0] == 1, "one batch row and one layer are supported"
    hg = d // 2
    xt = x.reshape(seq, d)

    c_col = c.reshape(d, 1)
    mod = _ada(c_col, w_ada[0], b_ada)
    mod_o = _ada(c_col, w_ada_out, b_ada_out.reshape(1, -1))
    sh_m, sc_m, gt_m, sh_f, sc_f, gt_f = [mod[:, s * d:(s + 1) * d] for s in range(6)]
    sh_o, sc_o = mod_o[:, :d], mod_o[:, d:]

    bias_in = jnp.concatenate([jnp.zeros((1, 4 * hg), F32), b_glu], axis=1)
    proj, = _norm_mod_matmul(xt, g_mix, sh_m, sc_m, w_in[0].astype(BF16), bias_in, emit_h=False)
    ycat = _hgrn2(proj, lb_logits, hg_norm_g, d_model=d)
    ycat = _conv_group(proj, w_dw[0], b_dw, ln_g, ln_b, ycat, d_model=d)
    x2 = _oproj(ycat, w_out[0].astype(BF16), xt, gt_m)

    q, h2 = _norm_mod_matmul(x2, g_ffn, sh_f, sc_f, w_pq[0].astype(BF16), jnp.zeros((1, w_pq.shape[2]), F32),
                             emit_h=True)
    s1, e1, s2, e2, tau = _router(q, sub_keys1[0], sub_keys2[0])
    y_ffn = _peer_dense(h2, expert_u[0].astype(BF16), expert_v[0].astype(BF16), s1, e1, s2, e2, tau)

    out = _final(x2, y_ffn, gt_f, g_out.reshape(1, d), sh_o, sc_o)
    return out.reshape(bsz, seq, d)
```

```python
import functools

import jax
import jax.numpy as jnp
from jax import lax
from jax.experimental import pallas as pl
from jax.experimental.pallas import tpu as pltpu

F32 = jnp.float32
BF16 = jnp.bfloat16

NORM_EPS = 1e-6
LN_EPS = 1e-5
LOG2_E = 1.4426950408889634
HEAD_DIM = 128
CONV_KERNEL = 31
CONV_HALO = 32
PEER_HEADS = 8
PEER_HALF = 128
N_KEYS = 128
PEER_TOPK = 16
LANES = 128
SUBLANES = 8
VMEM_LIMIT = 56 * 1024 * 1024

NT_DIMS = (((1,), (1,)), ((), ()))
TN_DIMS = (((0,), (0,)), ((), ()))


def _sigmoid(x):
    return 0.5 * jnp.tanh(0.5 * x) + 0.5


def _silu(x):
    return x * _sigmoid(x)


def _params(sem, flags=None):
    return pltpu.CompilerParams(dimension_semantics=sem, vmem_limit_bytes=VMEM_LIMIT, flags=flags)


def _tile(n, pref, unit=LANES):
    if n <= pref:
        return n
    best = unit
    for cand in range(unit, pref + 1, unit):
        if n % cand == 0:
            best = cand
    assert n % best == 0, (n, pref, unit)
    return best


def _ada_kernel(c_ref, w_ref, b_ref, o_ref, *, rows):
    d, tn = w_ref.shape

    def body(i, acc):
        r = pl.multiple_of(i * rows, rows)
        cc = _silu(c_ref[pl.ds(r, rows), :])
        blk = w_ref[pl.ds(r, rows), :] * cc
        return acc + jnp.sum(blk.reshape(rows // SUBLANES, SUBLANES, tn), axis=0)

    acc = lax.fori_loop(0, d // rows, body, jnp.zeros((SUBLANES, tn), F32))
    o_ref[...] = jnp.sum(acc, axis=0, keepdims=True) + b_ref[...]


def _ada(c_col, w, b, *, tn=512, rows=128):
    d, n = w.shape
    return pl.pallas_call(
        functools.partial(_ada_kernel, rows=rows),
        out_shape=jax.ShapeDtypeStruct((1, n), F32),
        grid=(n // tn,),
        in_specs=[
            pl.BlockSpec((d, 1), lambda j: (0, 0)),
            pl.BlockSpec((d, tn), lambda j: (0, j)),
            pl.BlockSpec((1, tn), lambda j: (0, j)),
        ],
        out_specs=pl.BlockSpec((1, tn), lambda j: (0, j)),
        compiler_params=_params(("parallel",)),
        name="ada",
    )(c_col, w, b)


def _norm_mod_rows(x_ref, g_ref, sh_ref, sc_ref, dst_ref, *, rows, res_ref=None, gate_ref=None):
    tm = x_ref.shape[0]
    g = g_ref[...]
    scale1 = 1.0 + sc_ref[...]
    sh = sh_ref[...]

    def body(i, carry):
        r = pl.multiple_of(i * rows, rows)
        x = x_ref[pl.ds(r, rows), :]
        if res_ref is not None:
            x = x + gate_ref[...] * res_ref[pl.ds(r, rows), :]
        ms = jnp.mean(x * x, axis=-1, keepdims=True)
        y = x * lax.rsqrt(ms + NORM_EPS) * g
        dst_ref[pl.ds(r, rows), :] = (y * scale1 + sh).astype(dst_ref.dtype)
        return carry

    lax.fori_loop(0, tm // rows, body, 0)


def _norm_mod_kernel(*refs, rows, with_res):
    if with_res:
        x_ref, y_ref, gate_ref, g_ref, sh_ref, sc_ref, o_ref = refs
        _norm_mod_rows(x_ref, g_ref, sh_ref, sc_ref, o_ref, rows=rows, res_ref=y_ref, gate_ref=gate_ref)
    else:
        x_ref, g_ref, sh_ref, sc_ref, o_ref = refs
        _norm_mod_rows(x_ref, g_ref, sh_ref, sc_ref, o_ref, rows=rows)


def _norm_mod(x, g, sh, sc, *, out_dtype, res=None, gate=None, tm=256, rows=32):
    t, d = x.shape
    tm = _tile(t, tm, SUBLANES)
    row = pl.BlockSpec((1, d), lambda i: (0, 0))
    blk = pl.BlockSpec((tm, d), lambda i: (i, 0))
    with_res = res is not None
    args = (x, res, gate, g, sh, sc) if with_res else (x, g, sh, sc)
    specs = [blk, blk, row, row, row, row] if with_res else [blk, row, row, row]
    return pl.pallas_call(
        functools.partial(_norm_mod_kernel, rows=rows, with_res=with_res),
        out_shape=jax.ShapeDtypeStruct((t, d), out_dtype),
        grid=(t // tm,),
        in_specs=specs,
        out_specs=blk,
        compiler_params=_params(("parallel",)),
        name="norm_mod",
    )(*args)


def _mm_kernel(x_ref, w_ref, b_ref, o_ref):
    o_ref[...] = jnp.dot(x_ref[...], w_ref[...], preferred_element_type=F32) + b_ref[...]


def _matmul_bias(x, w, b, *, tm=1024, tn=1024):
    t, d = x.shape
    n = w.shape[1]
    tm = _tile(t, tm, SUBLANES)
    tn = _tile(n, tn)
    return pl.pallas_call(
        _mm_kernel,
        out_shape=jax.ShapeDtypeStruct((t, n), F32),
        grid=(t // tm, n // tn),
        in_specs=[
            pl.BlockSpec((tm, d), lambda i, j: (i, 0)),
            pl.BlockSpec((d, tn), lambda i, j: (0, j)),
            pl.BlockSpec((1, tn), lambda i, j: (0, j)),
        ],
        out_specs=pl.BlockSpec((tm, tn), lambda i, j: (i, j)),
        compiler_params=_params(("parallel", "arbitrary")),
        name="matmul_bias",
    )(x, w, b)


def _anchor_rows(b, m):
    c = b.shape[0]
    blk = 2 * m
    if blk >= SUBLANES:
        a = b.reshape(c // blk, blk, HEAD_DIM)[:, m - 1:m, :]
        return jnp.broadcast_to(a, (c // blk, blk, HEAD_DIM)).reshape(c, HEAD_DIM)
    pos = lax.broadcasted_iota(jnp.int32, b.shape, 0) & (blk - 1)
    out = b
    for off in range(-(m - 1), m + 1):
        if off == 0:
            continue
        shifted = pltpu.roll(b, off % c, 0)
        out = jnp.where(pos == (m - 1 + off), shifted, out)
    return out


def _assemble_scores(level_scores, c):
    si = lax.broadcasted_iota(jnp.int32, (SUBLANES, LANES), 0)
    li = lax.broadcasted_iota(jnp.int32, (SUBLANES, LANES), 1)
    cache = {}

    def lane_range(lo, hi):
        if (lo, hi) not in cache:
            cache[(lo, hi)] = (li >= lo) & (li < hi) if lo > 0 else (li < hi)
        return cache[(lo, hi)]

    def in_group(m):
        if m not in cache:
            cache[m] = ((si ^ (li & (SUBLANES - 1))) < 2 * m) & ((si & m) != 0) & ((li & m) == 0)
        return cache[m]

    rows = []
    for rg in range(c // SUBLANES):
        r0 = rg * SUBLANES
        tiles = []
        for lt in range(c // LANES):
            c0 = lt * LANES
            val = jnp.zeros((SUBLANES, LANES), F32)
            for m, s_l in level_scores:
                tile = s_l[r0:r0 + SUBLANES, c0:c0 + LANES]
                if m < SUBLANES:
                    if not c0 <= r0 < c0 + LANES:
                        continue
                    g0 = r0 - c0
                    val = jnp.where(in_group(m) & lane_range(g0, g0 + SUBLANES), tile, val)
                    continue
                if not r0 & m:
                    continue
                start = r0 // (2 * m) * (2 * m)
                lo, hi = max(start, c0) - c0, min(start + m, c0 + LANES) - c0
                if lo >= hi:
                    continue
                val = tile if (lo, hi) == (0, LANES) else jnp.where(lane_range(lo, hi), tile, val)
            tiles.append(val)
        rows.append(tiles[0] if len(tiles) == 1 else jnp.concatenate(tiles, axis=1))
    return jnp.concatenate(rows, axis=0)


def _hgrn_chunk(q_raw, fz, v, g_raw, lb, ng, st):
    c = q_raw.shape[0]
    q = _silu(q_raw)
    f = lb + (1.0 - lb) * _sigmoid(fz)
    k = 1.0 - f
    row = lax.broadcasted_iota(jnp.int32, (c, HEAD_DIM), 0)

    p = jnp.log(f) * LOG2_E
    level_scores = []
    m = 1
    while m < c:
        lower = (row & m) != 0
        pa = _anchor_rows(p, m)
        dec = jnp.exp2(jnp.where(lower, p, pa - p))
        x = (jnp.where(lower, q, k) * dec).astype(BF16)
        level_scores.append((m, lax.dot_general(x, x, NT_DIMS, preferred_element_type=F32)))
        p = jnp.where(lower, p + pa, p)
        m *= 2
    scores = _assemble_scores(level_scores, c)

    vb = v.astype(BF16)
    o = jnp.sum(q * k, axis=-1, keepdims=True) * v
    o = o + jnp.dot(scores.astype(BF16), vb, preferred_element_type=F32)
    qd = (q * jnp.exp2(p)).astype(BF16)
    o = o + lax.dot_general(qd, st.astype(BF16), NT_DIMS, preferred_element_type=F32)

    plast = p[c - 1:c, :]
    kd = (k * jnp.exp2(plast - p)).astype(BF16)
    st_new = st * jnp.exp2(plast) + lax.dot_general(vb, kd, TN_DIMS, preferred_element_type=F32)

    o = o * lax.rsqrt(jnp.mean(o * o, axis=-1, keepdims=True) + NORM_EPS)
    y = o * ng * _silu(g_raw)
    return y, st_new


def _hgrn_kernel(q_ref, f_ref, i_ref, g_ref, lbl_ref, ng_ref, o_ref, st_ref, *, chunk):
    @pl.when(pl.program_id(1) == 0)
    def _():
        st_ref[...] = jnp.zeros_like(st_ref)

    lbl = lbl_ref[...]
    ex = jnp.exp(lbl - jnp.max(lbl, axis=0, keepdims=True))
    lb = ex[0:1, :] / jnp.sum(ex, axis=0, keepdims=True)
    ng = ng_ref[...]
    tb = q_ref.shape[0]

    def body(ci, carry):
        r = pl.multiple_of(ci * chunk, chunk)
        sl = pl.ds(r, chunk)
        y, st_new = _hgrn_chunk(q_ref[sl, :], f_ref[sl, :], i_ref[sl, :], g_ref[sl, :], lb, ng, st_ref[...])
        st_ref[...] = st_new
        o_ref[sl, :] = y.astype(o_ref.dtype)
        return carry

    lax.fori_loop(0, tb // chunk, body, 0, unroll=4)


def _hgrn2(proj, lb_logits, norm_g, *, d_model, tb=1024, chunk=256):
    t = proj.shape[0]
    heads = (d_model // 2) // HEAD_DIM
    tb = min(tb, t)
    chunk = min(chunk, tb)
    nl = lb_logits.shape[0]

    def col(off):
        return pl.BlockSpec((tb, HEAD_DIM), lambda h, i: (i, off * heads + h))

    return pl.pallas_call(
        functools.partial(_hgrn_kernel, chunk=chunk),
        out_shape=jax.ShapeDtypeStruct((t, d_model), BF16),
        grid=(heads, t // tb),
        in_specs=[
            col(0), col(1), col(2), col(3),
            pl.BlockSpec((nl, HEAD_DIM), lambda h, i: (0, h)),
            pl.BlockSpec((1, HEAD_DIM), lambda h, i: (0, h)),
        ],
        out_specs=pl.BlockSpec((tb, HEAD_DIM), lambda h, i: (i, h)),
        scratch_shapes=[pltpu.VMEM((HEAD_DIM, HEAD_DIM), F32)],
        compiler_params=_params(("parallel", "arbitrary")),
        name="hgrn2",
    )(proj, proj, proj, proj, lb_logits, norm_g)


def _conv_kernel(a_ref, b_ref, w_ref, bdw_ref, lng_ref, lnb_ref, ycat_ref, o_ref, ubuf, ybuf, *, rt, crt):
    del ycat_ref
    tb, cw = a_ref.shape
    i = pl.program_id(0)

    @pl.when(i == 0)
    def _():
        ubuf[0:CONV_HALO, :] = jnp.zeros((CONV_HALO, cw), F32)

    @pl.when(i > 0)
    def _():
        ubuf[0:CONV_HALO, :] = ubuf[tb:tb + CONV_HALO, :]

    def glu(ri, carry):
        r = pl.multiple_of(ri * rt, rt)
        ubuf[pl.ds(CONV_HALO + r, rt), :] = a_ref[pl.ds(r, rt), :] * _sigmoid(b_ref[pl.ds(r, rt), :])
        return carry

    lax.fori_loop(0, tb // rt, glu, 0)

    lead = CONV_HALO - (CONV_KERNEL - 1)

    def col_tile(ct, carry):
        c0 = pl.multiple_of(ct * LANES, LANES)
        cs = pl.ds(c0, LANES)
        w = w_ref[:, cs]
        bias = bdw_ref[:, cs]

        def row_tile(ri, carry2):
            r = pl.multiple_of(ri * crt, crt)
            win = ubuf[pl.ds(r, crt + CONV_HALO), cs]
            acc = jnp.zeros((crt, LANES), F32) + bias
            for phase in range(SUBLANES):
                bases = [off - phase for off in range(lead, lead + CONV_KERNEL) if off % SUBLANES == phase]
                shifted = win if phase == 0 else pltpu.roll(win, win.shape[0] - phase, 0)
                for base in bases:
                    j = base + phase - lead
                    acc = acc + w[j:j + 1, :] * shifted[base:base + crt, :]
            ybuf[pl.ds(r, crt), cs] = acc
            return carry2

        lax.fori_loop(0, tb // crt, row_tile, 0)
        return carry

    lax.fori_loop(0, cw // LANES, col_tile, 0)

    lng = lng_ref[...]
    lnb = lnb_ref[...]

    lrt = 16

    def ln(ri, carry):
        r = pl.multiple_of(ri * lrt, lrt)
        y = ybuf[pl.ds(r, lrt), :]
        mu = jnp.mean(y, axis=-1, keepdims=True)
        yc = y - mu
        var = jnp.mean(yc * yc, axis=-1, keepdims=True)
        yn = yc * lax.rsqrt(var + LN_EPS) * lng + lnb
        o_ref[pl.ds(r, lrt), :] = _silu(yn).astype(o_ref.dtype)
        return carry

    lax.fori_loop(0, tb // lrt, ln, 0, unroll=2)


def _conv_group(proj, w_dw, b_dw, ln_g, ln_b, ycat, *, d_model, tb=256, rt=32, crt=128):
    t = proj.shape[0]
    cw = d_model // 2
    tb = min(tb, t)
    row = pl.BlockSpec((1, cw), lambda i: (0, 0))
    return pl.pallas_call(
        functools.partial(_conv_kernel, rt=rt, crt=crt),
        out_shape=jax.ShapeDtypeStruct(ycat.shape, ycat.dtype),
        grid=(t // tb,),
        in_specs=[
            pl.BlockSpec((tb, cw), lambda i: (i, 4)),
            pl.BlockSpec((tb, cw), lambda i: (i, 5)),
            pl.BlockSpec((CONV_KERNEL, cw), lambda i: (0, 0)),
            row, row, row,
            pl.BlockSpec(memory_space=pl.ANY),
        ],
        out_specs=pl.BlockSpec((tb, cw), lambda i: (i, 1)),
        scratch_shapes=[pltpu.VMEM((tb + CONV_HALO, cw), F32), pltpu.VMEM((tb, cw), F32)],
        input_output_aliases={6: 0},
        compiler_params=_params(("arbitrary",)),
        name="conv",
    )(proj, proj, w_dw, b_dw, ln_g, ln_b, ycat)


def _oproj_kernel(y_ref, w_ref, x_ref, gate_ref, o_ref):
    acc = jnp.dot(y_ref[...], w_ref[...], preferred_element_type=F32)
    o_ref[...] = x_ref[...] + gate_ref[...] * acc


def _oproj(ycat, w, x, gate, *, tm=1024, tn=1024):
    t, kd = ycat.shape
    n = w.shape[1]
    tm = _tile(t, tm, SUBLANES)
    tn = _tile(n, tn)
    return pl.pallas_call(
        _oproj_kernel,
        out_shape=jax.ShapeDtypeStruct((t, n), F32),
        grid=(t // tm, n // tn),
        in_specs=[
            pl.BlockSpec((tm, kd), lambda i, j: (i, 0)),
            pl.BlockSpec((kd, tn), lambda i, j: (0, j)),
            pl.BlockSpec((tm, tn), lambda i, j: (i, j)),
            pl.BlockSpec((1, tn), lambda i, j: (0, j)),
        ],
        out_specs=pl.BlockSpec((tm, tn), lambda i, j: (i, j)),
        compiler_params=_params(("parallel", "arbitrary")),
        name="oproj",
    )(ycat, w, x, gate)


def _extract_desc(s, n):
    vals = []
    for _ in range(n):
        m = jnp.max(s, axis=0, keepdims=True)
        vals.append(m)
        s = jnp.where(s == m, -jnp.inf, s)
    return vals


def _router_kernel(q_ref, k1_ref, k2_ref, s1_ref, s2_ref, e2_ref, c1_ref, tau_ref):
    q = q_ref[...]
    s1 = lax.dot_general(k1_ref[0], q[:, :PEER_HALF], NT_DIMS, precision=lax.Precision.HIGHEST,
                         preferred_element_type=F32)
    s2 = lax.dot_general(k2_ref[0], q[:, PEER_HALF:], NT_DIMS, precision=lax.Precision.HIGHEST,
                         preferred_element_type=F32)
    n = PEER_TOPK + 1
    v1 = _extract_desc(s1, n)
    v2 = _extract_desc(s2, n)
    cand = [v1[a] + v2[b] for a in range(n) for b in range(n) if (a + 1) * (b + 1) <= n]
    pad = (-len(cand)) % SUBLANES
    cand += [jnp.full_like(cand[0], -jnp.inf)] * pad
    top = _extract_desc(jnp.concatenate(cand, axis=0), n)
    z = jnp.zeros_like(top[0])
    for m in top[:PEER_TOPK]:
        z = z + jnp.exp(m - top[0])
    s1_ref[0] = s1
    s2_ref[0] = s2
    e2_ref[0] = jnp.exp(s2 - v2[0])
    c1_ref[0] = v1[0] + jnp.log(z)
    tau_ref[0] = 0.5 * (top[PEER_TOPK - 1] + top[PEER_TOPK])


def _router(q, keys1, keys2, *, tt=1024):
    t = q.shape[0]
    tt = min(tt, t)
    big = jax.ShapeDtypeStruct((PEER_HEADS, N_KEYS, t), F32)
    row = jax.ShapeDtypeStruct((PEER_HEADS, 1, t), F32)
    key_spec = pl.BlockSpec((1, N_KEYS, PEER_HALF), lambda i, h: (h, 0, 0))
    big_spec = pl.BlockSpec((1, N_KEYS, tt), lambda i, h: (h, 0, i))
    row_spec = pl.BlockSpec((1, 1, tt), lambda i, h: (h, 0, i))
    return pl.pallas_call(
        _router_kernel,
        out_shape=[big, big, big, row, row],
        grid=(t // tt, PEER_HEADS),
        in_specs=[pl.BlockSpec((tt, 2 * PEER_HALF), lambda i, h: (i, h)), key_spec, key_spec],
        out_specs=[big_spec, big_spec, big_spec, row_spec, row_spec],
        compiler_params=_params(("parallel", "parallel")),
        name="router",
    )(q, keys1, keys2)


def _peer_gate(a_t, s1_ref, s2_ref, e2_ref, c1_ref, tau_ref, i1_base):
    n_i1 = a_t.shape[0] // N_KEYS
    act = 0.5 * a_t * (1.0 + lax.erf(a_t * 0.7071067811865476))
    parts = []
    for il in range(n_i1):
        w = jnp.zeros((N_KEYS, a_t.shape[1]), F32)
        for hh in range(PEER_HEADS):
            s1row = s1_ref[hh, pl.ds(i1_base + il, 1), :]
            e1row = jnp.exp(s1row - c1_ref[hh])
            thr = tau_ref[hh] - s1row
            w = w + jnp.where(s2_ref[hh] > thr, e1row * e2_ref[hh], 0.0)
        parts.append(w)
    wt = parts[0] if n_i1 == 1 else jnp.concatenate(parts, axis=0)
    return (act * wt).astype(BF16)


def _peer_dense_kernel(h_ref, u_ref, v_ref, s1_ref, s2_ref, e2_ref, c1_ref, tau_ref, y_ref, *, sub):
    eb = pl.program_id(1)
    n_exp = u_ref.shape[0]

    @pl.when(eb == 0)
    def _():
        y_ref[...] = jnp.zeros_like(y_ref)

    starts = list(range(0, n_exp, sub))
    a_ts = [lax.dot_general(u_ref[s:s + sub, :], h_ref[...], NT_DIMS, preferred_element_type=F32)
            for s in starts]
    for s, a_t in zip(starts, a_ts):
        i1_base = (eb * n_exp + s) // N_KEYS
        g_t = _peer_gate(a_t, s1_ref, s2_ref, e2_ref, c1_ref, tau_ref, i1_base)
        y_ref[...] += lax.dot_general(g_t, v_ref[s:s + sub, :], TN_DIMS, preferred_element_type=F32)


def _peer_dense(h, u, v, s1, s2, e2, c1, tau, *, tb=512, eb=512, sub=256):
    t, d = h.shape
    n_exp = u.shape[0]
    tb = min(tb, t)
    once = pl.Buffered(1)
    big_spec = pl.BlockSpec((PEER_HEADS, N_KEYS, tb), lambda i, e: (0, 0, i), pipeline_mode=once)
    row_spec = pl.BlockSpec((PEER_HEADS, 1, tb), lambda i, e: (0, 0, i), pipeline_mode=once)
    return pl.pallas_call(
        functools.partial(_peer_dense_kernel, sub=sub),
        out_shape=jax.ShapeDtypeStruct((t, d), F32),
        grid=(t // tb, n_exp // eb),
        in_specs=[
            pl.BlockSpec((tb, d), lambda i, e: (i, 0), pipeline_mode=once),
            pl.BlockSpec((eb, d), lambda i, e: (e, 0)),
            pl.BlockSpec((eb, d), lambda i, e: (e, 0)),
            big_spec, big_spec, big_spec, row_spec, row_spec,
        ],
        out_specs=pl.BlockSpec((tb, d), lambda i, e: (i, 0)),
        compiler_params=_params(("parallel", "arbitrary")),
        name="peer_dense",
    )(h, u, v, s1, s2, e2, c1, tau)


def kernel(x, c, w_ada, b_ada, w_ada_out, b_ada_out, g_mix, g_ffn, g_out, w_in, b_glu, lb_logits, w_dw, b_dw,
           ln_g, ln_b, hg_norm_g, w_out, w_pq, sub_keys1, sub_keys2, expert_u, expert_v):
    bsz, seq, d = x.shape
    assert bsz == 1 and w_ada.shape[0] == 1, "one batch row and one layer are supported"
    hg = d // 2
    xt = x.reshape(seq, d)

    c_col = c.reshape(d, 1)
    mod = _ada(c_col, w_ada[0], b_ada)
    mod_o = _ada(c_col, w_ada_out, b_ada_out.reshape(1, -1))
    sh_m, sc_m, gt_m, sh_f, sc_f, gt_f = [mod[:, s * d:(s + 1) * d] for s in range(6)]
    sh_o, sc_o = mod_o[:, :d], mod_o[:, d:]

    bias_in = jnp.concatenate([jnp.zeros((1, 4 * hg), F32), b_glu], axis=1)
    h1 = _norm_mod(xt, g_mix, sh_m, sc_m, out_dtype=BF16)
    proj = _matmul_bias(h1, w_in[0].astype(BF16), bias_in)
    ycat = _hgrn2(proj, lb_logits, hg_norm_g, d_model=d)
    ycat = _conv_group(proj, w_dw[0], b_dw, ln_g, ln_b, ycat, d_model=d)
    x2 = _oproj(ycat, w_out[0].astype(BF16), xt, gt_m)

    h2 = _norm_mod(x2, g_ffn, sh_f, sc_f, out_dtype=BF16)
    q = _matmul_bias(h2, w_pq[0].astype(BF16), jnp.zeros((1, w_pq.shape[2]), F32), tn=2048)
    s1, s2, e2, c1, tau = _router(q, sub_keys1[0], sub_keys2[0])
    y_ffn = _peer_dense(h2, expert_u[0].astype(BF16), expert_v[0].astype(BF16), s1, s2, e2, c1, tau)

    out = _norm_mod(x2, g_out.reshape(1, d), sh_o, sc_o, out_dtype=F32, res=y_ffn, gate=gt_f)
    return out.reshape(bsz, seq, d)
```
